```python
import jax, jax.numpy as jnp
from jax import lax
import numpy as np

D_MODEL = 4096
BATCH = 4
SEQ = 2048
DEPTH = 2
DEC_BATCH = 128
DEC_SEQ = 4
PAST_LEN = 16384
PAGE_SIZE = 128

MLA_HEADS = D_MODEL // 256
Q_LORA = D_MODEL // 4
KV_LORA = 256
QK_NOPE = 128
QK_ROPE = 64
V_DIM = 128
ROPE_THETA = 10000.0
SB_HEADS = D_MODEL // 256
SB_HEAD_DIM = 128
D_FF = 4 * D_MODEL
Q_BLOCK = 128
DN_ALPHA = (2 * DEPTH) ** 0.25
DN_BETA = (8 * DEPTH) ** -0.25
LN_EPS = 1e-5
RMS_EPS = 1e-6
MLA_SCALE = (QK_NOPE + QK_ROPE) ** -0.5
SB_SCALE = SB_HEAD_DIM ** -0.5
MLA_CACHE_DIM = KV_LORA + QK_ROPE
SB_CACHE_DIM = 2 * SB_HEAD_DIM
IN_SPLITS = (Q_LORA, KV_LORA, QK_ROPE, SB_HEADS * SB_HEAD_DIM, SB_HEAD_DIM, SB_HEAD_DIM, D_MODEL, D_MODEL)
IN_COLS = sum(IN_SPLITS)
IN_OFFSETS = tuple(sum(IN_SPLITS[:i + 1]) for i in range(len(IN_SPLITS) - 1))

kernel_name = 'mla_stickbreak_gated_deepnorm_step'


def layer_norm(x, g, b):
    xf = x.astype(jnp.float32)
    mu = xf.mean(-1, keepdims=True)
    var = jnp.square(xf - mu).mean(-1, keepdims=True)
    y = (xf - mu) * lax.rsqrt(var + LN_EPS) * g.astype(jnp.float32) + b.astype(jnp.float32)
    return y.astype(x.dtype)


def rms_norm(x, g):
    xf = x.astype(jnp.float32)
    y = xf * lax.rsqrt(jnp.square(xf).mean(-1, keepdims=True) + RMS_EPS) * g.astype(jnp.float32)
    return y.astype(x.dtype)


def rope(x, pos):
    half = QK_ROPE // 2
    inv_freq = 1.0 / (ROPE_THETA ** (jnp.arange(0, QK_ROPE, 2, dtype=jnp.float32) / QK_ROPE))
    ang = pos.astype(jnp.float32)[:, None] * inv_freq[None, :]
    cos, sin = jnp.cos(ang).astype(x.dtype), jnp.sin(ang).astype(x.dtype)
    x1, x2 = x[..., :half], x[..., half:]
    return jnp.concatenate([x1 * cos - x2 * sin, x2 * cos + x1 * sin], axis=-1)


def project_inputs(x, pos, w_in, q_norm_g, w_uq, kv_norm_g, w_uk):
    B, T = x.shape[:2]
    z = jnp.einsum('btd,dc->btc', x, w_in)
    cq, ckv, kr, sq, sk, sv, ga, gb = jnp.split(z, IN_OFFSETS, axis=-1)
    q = jnp.einsum('btc,chd->bhtd', rms_norm(cq, q_norm_g), w_uq)
    q_lat = jnp.einsum('bhtd,chd->bhtc', q[..., :QK_NOPE], w_uk)
    q_rope = rope(q[..., QK_NOPE:], pos)
    ckv = rms_norm(ckv, kv_norm_g)
    kr = rope(kr, pos)
    sq = sq.reshape(B, T, SB_HEADS, SB_HEAD_DIM).transpose(0, 2, 1, 3)
    return q_lat, q_rope, ckv, kr, sq, sk, sv, jax.nn.sigmoid(ga), jax.nn.sigmoid(gb)


def mla_block(q_lat, q_rope, ckv, kr, q_pos, k_pos):
    s = jnp.einsum('bhtc,bsc->bhts', q_lat, ckv) + jnp.einsum('bhtr,bsr->bhts', q_rope, kr)
    s = s.astype(jnp.float32) * MLA_SCALE
    mask = k_pos[None, :] <= q_pos[:, None]
    p = jax.nn.softmax(jnp.where(mask, s, -jnp.inf), axis=-1).astype(ckv.dtype)
    return jnp.einsum('bhts,bsc->bhtc', p, ckv)


def sb_block(q, k, v, q_pos, k_pos):
    z = jnp.einsum('bhtd,bsd->bhts', q, k).astype(jnp.float32) * SB_SCALE
    mask = k_pos[None, :] < q_pos[:, None]
    log_beta = jnp.where(mask, jax.nn.log_sigmoid(z), -jnp.inf)
    log_keep = jnp.where(mask, jax.nn.log_sigmoid(-z), 0.0)
    later = lax.cumsum(log_keep, axis=3, reverse=True) - log_keep
    a = jnp.exp(log_beta + later).astype(v.dtype)
    return jnp.einsum('bhts,bsd->bhtd', a, v)


def prompt_attention(q_lat, q_rope, ckv, kr, sq, sk, sv):
    S = q_lat.shape[2]
    n_blk = S // Q_BLOCK
    pos = jnp.arange(S, dtype=jnp.int32)

    def to_blocks(a):
        return a.reshape(a.shape[0], a.shape[1], n_blk, Q_BLOCK, a.shape[-1]).transpose(2, 0, 1, 3, 4)

    def from_blocks(a):
        return a.transpose(1, 2, 0, 3, 4).reshape(a.shape[1], a.shape[2], S, a.shape[-1])

    def body(blk):
        ql, qr, qs, qp = blk
        return mla_block(ql, qr, ckv, kr, qp, pos), sb_block(qs, sk, sv, qp, pos)

    o_lat, o_sb = lax.map(body, (to_blocks(q_lat), to_blocks(q_rope), to_blocks(sq), pos.reshape(n_blk, Q_BLOCK)))
    return from_blocks(o_lat), from_blocks(o_sb)


def merge_branches(o_lat, o_sb, ga, gb, w_uv, w_pa, w_pb, w_o):
    B, _, T, _ = o_lat.shape
    o_a = jnp.einsum('bhtc,chd->bthd', o_lat, w_uv).reshape(B, T, MLA_HEADS * V_DIM) @ w_pa
    o_b = o_sb.transpose(0, 2, 1, 3).reshape(B, T, SB_HEADS * SB_HEAD_DIM) @ w_pb
    return (ga * o_a + gb * o_b) @ w_o


def post_sublayers(x, mix, ln1_g, ln1_b, w_up, w_down, ln2_g, ln2_b):
    h = layer_norm(DN_ALPHA * x + mix, ln1_g, ln1_b)
    f = jnp.square(jax.nn.relu(h @ w_up)) @ w_down
    return layer_norm(DN_ALPHA * h + f, ln2_g, ln2_b)


def setup_inputs(seed: int = 0) -> dict:
    key = jax.random.key(seed)
    ks = jax.random.split(key, 24)
    n_pages = PAST_LEN // PAGE_SIZE
    n_used = DEC_BATCH * n_pages
    n_pool = n_used + n_used // 4

    def nrm(k, shape, scale):
        return jax.random.normal(k, shape, jnp.float32) * scale

    col_scale = jnp.ones((IN_COLS,), jnp.float32).at[IN_OFFSETS[5]:IN_OFFSETS[6]].set(DN_BETA)
    page_table = jax.random.permutation(ks[4], n_pool)[:n_used].reshape(DEC_BATCH, n_pages).astype(jnp.int32)
    return {
        'x_prompt': nrm(ks[0], (BATCH, SEQ, D_MODEL), 1.0),
        'x_sample': nrm(ks[1], (DEC_BATCH, DEC_SEQ, D_MODEL), 1.0),
        'cache_mla': nrm(ks[2], (DEPTH, n_pool, PAGE_SIZE, MLA_CACHE_DIM), 1.0),
        'cache_sb': nrm(ks[3], (DEPTH, n_pool, PAGE_SIZE, SB_CACHE_DIM), 1.0),
        'page_table': page_table,
        'w_in': nrm(ks[5], (DEPTH, D_MODEL, IN_COLS), D_MODEL ** -0.5) * col_scale,
        'q_norm_g': 1.0 + nrm(ks[6], (DEPTH, Q_LORA), 0.02),
        'w_uq': nrm(ks[7], (DEPTH, Q_LORA, MLA_HEADS, QK_NOPE + QK_ROPE), Q_LORA ** -0.5),
        'kv_norm_g': 1.0 + nrm(ks[8], (DEPTH, KV_LORA), 0.02),
        'w_uk': nrm(ks[9], (DEPTH, KV_LORA, MLA_HEADS, QK_NOPE), KV_LORA ** -0.5),
        'w_uv': nrm(ks[10], (DEPTH, KV_LORA, MLA_HEADS, V_DIM), KV_LORA ** -0.5 * DN_BETA),
        'w_pa': nrm(ks[11], (DEPTH, MLA_HEADS * V_DIM, D_MODEL), (MLA_HEADS * V_DIM) ** -0.5 * DN_BETA),
        'w_pb': nrm(ks[12], (DEPTH, SB_HEADS * SB_HEAD_DIM, D_MODEL), (SB_HEADS * SB_HEAD_DIM) ** -0.5 * DN_BETA),
        'w_o': nrm(ks[13], (DEPTH, D_MODEL, D_MODEL), D_MODEL ** -0.5 * DN_BETA),
        'ln1_g': 1.0 + nrm(ks[14], (DEPTH, D_MODEL), 0.02),
        'ln1_b': nrm(ks[15], (DEPTH, D_MODEL), 0.02),
        'w_up': nrm(ks[16], (DEPTH, D_MODEL, D_FF), D_MODEL ** -0.5),
        'w_down': nrm(ks[17], (DEPTH, D_FF, D_MODEL), D_FF ** -0.5 * DN_BETA),
        'ln2_g': 1.0 + nrm(ks[18], (DEPTH, D_MODEL), 0.02),
        'ln2_b': nrm(ks[19], (DEPTH, D_MODEL), 0.02),
    }


def reference(x_prompt, x_sample, cache_mla, cache_sb, page_table, w_in, q_norm_g, w_uq, kv_norm_g, w_uk,
              w_uv, w_pa, w_pb, w_o, ln1_g, ln1_b, w_up, w_down, ln2_g, ln2_b):
    S = x_prompt.shape[1]
    n_dec, T = x_sample.shape[:2]
    past_len = page_table.shape[1] * cache_mla.shape[2]
    pos_p = jnp.arange(S, dtype=jnp.int32)
    pos_s = past_len + jnp.arange(T, dtype=jnp.int32)
    key_pos_s = jnp.arange(past_len + T, dtype=jnp.int32)
    xp, xs = x_prompt, x_sample
    rows_mla_p, rows_sb_p, rows_mla_s, rows_sb_s = [], [], [], []
    for l in range(DEPTH):
        proj = (w_in[l], q_norm_g[l], w_uq[l], kv_norm_g[l], w_uk[l])
        outw = (w_uv[l], w_pa[l], w_pb[l], w_o[l])
        post = (ln1_g[l], ln1_b[l], w_up[l], w_down[l], ln2_g[l], ln2_b[l])
        ql, qr, ckv, kr, sq, sk, sv, ga, gb = project_inputs(xp, pos_p, *proj)
        o_lat, o_sb = prompt_attention(ql, qr, ckv, kr, sq, sk, sv)
        rows_mla_p.append(jnp.concatenate([ckv, kr], axis=-1))
        rows_sb_p.append(jnp.concatenate([sk, sv], axis=-1))
        xp = post_sublayers(xp, merge_branches(o_lat, o_sb, ga, gb, *outw), *post)
        ql, qr, ckv, kr, sq, sk, sv, ga, gb = project_inputs(xs, pos_s, *proj)
        new_mla = jnp.concatenate([ckv, kr], axis=-1)
        new_sb = jnp.concatenate([sk, sv], axis=-1)
        past_mla = jnp.take(cache_mla[l], page_table, axis=0).reshape(n_dec, past_len, MLA_CACHE_DIM)
        past_sb = jnp.take(cache_sb[l], page_table, axis=0).reshape(n_dec, past_len, SB_CACHE_DIM)
        keys_mla = jnp.concatenate([past_mla, new_mla], axis=1)
        keys_sb = jnp.concatenate([past_sb, new_sb], axis=1)
        o_lat = mla_block(ql, qr, keys_mla[..., :KV_LORA], keys_mla[..., KV_LORA:], pos_s, key_pos_s)
        o_sb = sb_block(sq, keys_sb[..., :SB_HEAD_DIM], keys_sb[..., SB_HEAD_DIM:], pos_s, key_pos_s)
        rows_mla_s.append(new_mla)
        rows_sb_s.append(new_sb)
        xs = post_sublayers(xs, merge_branches(o_lat, o_sb, ga, gb, *outw), *post)
    return (xp, xs, jnp.stack(rows_mla_p), jnp.stack(rows_sb_p), jnp.stack(rows_mla_s), jnp.stack(rows_sb_s))
```

```python
import functools

import jax
import jax.numpy as jnp
from jax import lax
from jax.experimental import pallas as pl
from jax.experimental.pallas import tpu as pltpu

F32 = jnp.float32
BF16 = jnp.bfloat16

ROPE_THETA = 10000.0
LN_EPS = 1e-5
RMS_EPS = 1e-6
NEG_BIG = -1e30

LANE = 128
Q_TILE = 128
K_TILE = 256
VMEM_LIMIT = 56 * 1024 * 1024


def _cparams(n_axes):
    return pltpu.CompilerParams(dimension_semantics=("arbitrary",) * n_axes, vmem_limit_bytes=VMEM_LIMIT)


def _pick(n, candidates):
    for c in candidates:
        if n % c == 0:
            return c
    raise ValueError(f"no block size among {candidates} divides {n}")


def _dot(a, b):
    return jnp.dot(a, b, preferred_element_type=F32)


def _dot_nt(a, b):
    return lax.dot_general(a, b, (((1,), (1,)), ((), ())), preferred_element_type=F32)


def _mm_kernel(x_ref, w_ref, o_ref, *, act):
    y = _dot(x_ref[...], w_ref[...])
    if act == "relu2":
        y = jnp.square(jnp.maximum(y, 0.0))
    o_ref[...] = y.astype(o_ref.dtype)


def _mm_acc_kernel(x_ref, w_ref, o_ref, acc_ref):
    k = pl.program_id(2)

    @pl.when(k == 0)
    def _():
        acc_ref[...] = jnp.zeros_like(acc_ref)

    acc_ref[...] += _dot(x_ref[...], w_ref[...])

    @pl.when(k == pl.num_programs(2) - 1)
    def _():
        o_ref[...] = acc_ref[...].astype(o_ref.dtype)


def _mm(x, w, out_dtype, act=None, name="mm"):
    M, K = x.shape
    N = w.shape[1]
    bm = _pick(M, (512, 256, 128, 64, 32, 16, 8))
    bn = _pick(N, (1024, 512, 256, 128))
    bk = 4096
    if K <= bk:
        return pl.pallas_call(
            functools.partial(_mm_kernel, act=act),
            grid=(N // bn, M // bm),
            in_specs=[pl.BlockSpec((bm, K), lambda n, m: (m, 0)),
                      pl.BlockSpec((K, bn), lambda n, m: (0, n))],
            out_specs=pl.BlockSpec((bm, bn), lambda n, m: (m, n)),
            out_shape=jax.ShapeDtypeStruct((M, N), out_dtype),
            compiler_params=_cparams(2), name=name)(x, w)
    assert act is None and K % bk == 0
    return pl.pallas_call(
        _mm_acc_kernel,
        grid=(N // bn, M // bm, K // bk),
        in_specs=[pl.BlockSpec((bm, bk), lambda n, m, k: (m, k)),
                  pl.BlockSpec((bk, bn), lambda n, m, k: (k, n))],
        out_specs=pl.BlockSpec((bm, bn), lambda n, m, k: (m, n)),
        out_shape=jax.ShapeDtypeStruct((M, N), out_dtype),
        scratch_shapes=[pltpu.VMEM((bm, bn), F32)],
        compiler_params=_cparams(3), name=name)(x, w)


def _rope_lanes(x, cos, sin):
    lane = lax.broadcasted_iota(jnp.int32, x.shape, 1)
    swapped = jnp.where(lane < 32, pltpu.roll(x, 96, 1), pltpu.roll(x, 32, 1))
    return x * cos + swapped * sin


def _rows_kernel(z_ref, gq_ref, gkv_ref, cos_ref, sin_ref,
                 cqn_ref, mla_ref, kmla_ref, sb_ref, ksb_ref, *, ql, c):
    cq = z_ref[:, :ql]
    cqn = cq * lax.rsqrt(jnp.mean(jnp.square(cq), axis=-1, keepdims=True) + RMS_EPS) * gq_ref[...]
    cqn_ref[...] = cqn.astype(BF16)
    ckv = z_ref[:, ql:ql + c]
    ckvn = ckv * lax.rsqrt(jnp.mean(jnp.square(ckv), axis=-1, keepdims=True) + RMS_EPS) * gkv_ref[...]
    kr = _rope_lanes(z_ref[:, ql + c:ql + c + LANE], cos_ref[...], sin_ref[...])[:, :64]
    mla_ref[:, :c] = ckvn
    mla_ref[:, c:] = kr
    kmla_ref[:, :c] = ckvn.astype(BF16)
    kmla_ref[:, c:] = kr.astype(BF16)
    sb = z_ref[:, ql + c + LANE:ql + c + 3 * LANE]
    sb_ref[...] = sb
    ksb_ref[...] = sb.astype(BF16)


def _rows(z_a, gq, gkv, cos_t, sin_t, ql, c):
    M, wa = z_a.shape
    tm = _pick(M, (256, 128, 64, 32, 16, 8))
    row = lambda w: pl.BlockSpec((tm, w), lambda i: (i, 0))
    full = lambda w: pl.BlockSpec((1, w), lambda i: (0, 0))
    return pl.pallas_call(
        functools.partial(_rows_kernel, ql=ql, c=c),
        grid=(M // tm,),
        in_specs=[row(wa), full(ql), full(c), row(LANE), row(LANE)],
        out_specs=[row(ql), row(c + 64), row(c + 64), row(2 * LANE), row(2 * LANE)],
        out_shape=[jax.ShapeDtypeStruct((M, ql), BF16),
                   jax.ShapeDtypeStruct((M, c + 64), F32),
                   jax.ShapeDtypeStruct((M, c + 64), BF16),
                   jax.ShapeDtypeStruct((M, 2 * LANE), F32),
                   jax.ShapeDtypeStruct((M, 2 * LANE), BF16)],
        compiler_params=_cparams(1), name="rows")(z_a, gq, gkv, cos_t, sin_t)


def _qbuild_kernel(q_ref, wuk_ref, cos_ref, sin_ref, qlat_ref, qrope_ref, *, heads, nope, c):
    cos = cos_ref[...]
    sin = sin_ref[...]
    for h in range(heads):
        qn = q_ref[:, h * nope:(h + 1) * nope].astype(BF16)
        qlat_ref[:, h * c:(h + 1) * c] = _dot(qn, wuk_ref[h]).astype(BF16)
        qr = q_ref[:, heads * nope + h * LANE:heads * nope + (h + 1) * LANE]
        qrope_ref[:, h * LANE:(h + 1) * LANE] = _rope_lanes(qr, cos, sin).astype(BF16)


def _qbuild(q, w_ukt, cos_t, sin_t, heads, nope, c):
    M = q.shape[0]
    tm = _pick(M, (256, 128, 64, 32, 16, 8))
    row = lambda w: pl.BlockSpec((tm, w), lambda i: (i, 0))
    return pl.pallas_call(
        functools.partial(_qbuild_kernel, heads=heads, nope=nope, c=c),
        grid=(M // tm,),
        in_specs=[row(q.shape[1]), pl.BlockSpec(w_ukt.shape, lambda i: (0, 0, 0)), row(LANE), row(LANE)],
        out_specs=[row(heads * c), row(heads * LANE)],
        out_shape=[jax.ShapeDtypeStruct((M, heads * c), BF16),
                   jax.ShapeDtypeStruct((M, heads * LANE), BF16)],
        compiler_params=_cparams(1), name="qbuild")(q, w_ukt, cos_t, sin_t)


def _suffix_matrix(n):
    j = lax.broadcasted_iota(jnp.int32, (n, n), 0)
    s = lax.broadcasted_iota(jnp.int32, (n, n), 1)
    return (j > s).astype(BF16)


def _suffix_sum(x, u):
    hi = x.astype(BF16)
    lo = (x - hi.astype(F32)).astype(BF16)
    return _dot(hi, u) + _dot(lo, u)


def _sb_logs(z, mask):
    t = jnp.log1p(jnp.exp(-jnp.abs(z)))
    log_beta = jnp.minimum(z, 0.0) - t
    log_keep = -jnp.maximum(z, 0.0) - t
    if mask is not None:
        log_keep = jnp.where(mask, log_keep, 0.0)
    return log_beta, log_keep


def _prompt_attn_kernel(qlat_ref, qrope_ref, sq_ref, kmla_ref, ksb_ref, wuv_ref, oa_ref, osb_ref,
                        acc_ref, accsb_ref, *, hg, c, mla_scale, sb_scale):
    i = pl.program_id(1)
    tq, tk = Q_TILE, K_TILE
    mq = hg * tq
    q_lat = jnp.concatenate([qlat_ref[:, h * c:(h + 1) * c] for h in range(hg)], axis=0)
    q_rope = jnp.concatenate([qrope_ref[:, h * LANE:h * LANE + 64] for h in range(hg)], axis=0)
    q_sb = jnp.concatenate([sq_ref[:, h * LANE:(h + 1) * LANE] for h in range(hg)], axis=0)
    q_pos = i * tq + (lax.broadcasted_iota(jnp.int32, (mq, 1), 0) & (tq - 1))
    n_kb = (i * tq + tq - 1) // tk + 1
    u = _suffix_matrix(tk)
    acc_ref[...] = jnp.zeros_like(acc_ref)
    accsb_ref[...] = jnp.zeros_like(accsb_ref)

    def body(it, carry):
        m, l, run = carry
        start = pl.multiple_of((n_kb - 1 - it) * tk, tk)
        k_pos = start + lax.broadcasted_iota(jnp.int32, (1, tk), 1)
        s = _dot_nt(q_lat, kmla_ref[pl.ds(start, tk), 0:c]) + _dot_nt(q_rope, kmla_ref[pl.ds(start, tk), c:c + 64])
        s = jnp.where(k_pos <= q_pos, s * mla_scale, NEG_BIG)
        m_new = jnp.maximum(m, jnp.max(s, axis=-1, keepdims=True))
        alpha = jnp.exp(m - m_new)
        p = jnp.exp(s - m_new)
        l = alpha * l + jnp.sum(p, axis=-1, keepdims=True)
        acc_ref[...] = alpha * acc_ref[...] + _dot(p.astype(BF16), kmla_ref[pl.ds(start, tk), 0:c])
        mask = k_pos < q_pos
        z = _dot_nt(q_sb, ksb_ref[pl.ds(start, tk), 0:LANE]) * sb_scale
        log_beta, log_keep = _sb_logs(z, mask)
        later = _suffix_sum(log_keep, u) + run
        a = jnp.where(mask, jnp.exp(log_beta + later), 0.0)
        accsb_ref[...] += _dot(a.astype(BF16), ksb_ref[pl.ds(start, tk), LANE:2 * LANE])
        run = run + jnp.sum(log_keep, axis=-1, keepdims=True)
        return m_new, l, run

    init = (jnp.full((mq, 1), NEG_BIG, F32), jnp.zeros((mq, 1), F32), jnp.zeros((mq, 1), F32))
    _, l, _ = lax.fori_loop(0, n_kb, body, init)
    o_lat = (acc_ref[...] / l).astype(BF16)
    for h in range(hg):
        oa_ref[:, h * LANE:(h + 1) * LANE] = _dot(o_lat[h * tq:(h + 1) * tq], wuv_ref[h]).astype(BF16)
        osb_ref[:, h * LANE:(h + 1) * LANE] = accsb_ref[h * tq:(h + 1) * tq, :].astype(BF16)


def _prompt_attention(q_lat, q_rope, sq, kmla, ksb, w_uvh, batch, seq, heads, c, mla_scale, sb_scale):
    hg = 4 if heads % 4 == 0 else 1
    nq = seq // Q_TILE
    assert seq % K_TILE == 0
    qspec = lambda w: pl.BlockSpec((Q_TILE, hg * w), lambda b, i, g: (b * nq + i, g))
    kspec = lambda w: pl.BlockSpec((seq, w), lambda b, i, g: (b, 0))
    return pl.pallas_call(
        functools.partial(_prompt_attn_kernel, hg=hg, c=c, mla_scale=mla_scale, sb_scale=sb_scale),
        grid=(batch, nq, heads // hg),
        in_specs=[qspec(c), qspec(LANE), qspec(LANE), kspec(c + 64), kspec(2 * LANE),
                  pl.BlockSpec((hg, c, LANE), lambda b, i, g: (g, 0, 0))],
        out_specs=[qspec(LANE), qspec(LANE)],
        out_shape=[jax.ShapeDtypeStruct((batch * seq, heads * LANE), BF16)] * 2,
        scratch_shapes=[pltpu.VMEM((hg * Q_TILE, c), F32), pltpu.VMEM((hg * Q_TILE, LANE), F32)],
        compiler_params=_cparams(3), name="prompt_attn")(q_lat, q_rope, sq, kmla, ksb, w_uvh)


def _sample_attn_kernel(pt_ref, qlat_ref, qrope_ref, sq_ref, newk_ref, newsb_ref, *rest,
                        pps, c, t_new, mla_scale, sb_scale):
    del pt_ref
    mla_pages = rest[:pps]
    sb_pages = rest[pps:2 * pps]
    olat_ref, osb_ref = rest[2 * pps:2 * pps + 2]
    kbuf, sbuf, m_ref, l_ref, run_ref, acc_ref, accsb_ref = rest[2 * pps + 2:]
    ch = pl.program_id(1)
    page = LANE
    q_lat = qlat_ref[...]
    q_rope = qrope_ref[:, 0:64]
    q_sb = sq_ref[...]
    rows = q_lat.shape[0]
    u = _suffix_matrix(page)

    def update(k_lat, k_rope, sb_k, sb_v, n, mla_mask, sb_mask):
        s = (_dot_nt(q_lat, k_lat) + _dot_nt(q_rope, k_rope)) * mla_scale
        if mla_mask is not None:
            s = jnp.where(mla_mask, s, NEG_BIG)
        m = m_ref[...]
        m_new = jnp.maximum(m, jnp.max(s, axis=-1, keepdims=True))
        alpha = jnp.exp(m - m_new)
        p = jnp.exp(s - m_new)
        l_ref[...] = alpha * l_ref[...] + jnp.sum(p, axis=-1, keepdims=True)
        m_ref[...] = m_new
        acc_ref[...] = alpha * acc_ref[...] + _dot(p.astype(BF16), k_lat)
        z = jnp.concatenate([_dot_nt(q_sb, sb_k[j * page:(j + 1) * page]) for j in range(n)], axis=0) * sb_scale
        log_beta, log_keep = _sb_logs(z, sb_mask)
        within = _suffix_sum(log_keep, u)
        tot = jnp.sum(log_keep, axis=-1, keepdims=True)
        run = run_ref[...]
        runs = []
        for j in range(n):
            runs.append(run)
            run = run + tot[j * rows:(j + 1) * rows]
        run_ref[...] = run
        a = jnp.exp(log_beta + within + jnp.concatenate(runs, axis=0))
        if sb_mask is not None:
            a = jnp.where(sb_mask, a, 0.0)
        a = a.astype(BF16)
        upd = _dot(a[0:rows], sb_v[0:page])
        for j in range(1, n):
            upd += _dot(a[j * rows:(j + 1) * rows], sb_v[j * page:(j + 1) * page])
        accsb_ref[...] += upd

    @pl.when(ch == 0)
    def _():
        m_ref[...] = jnp.full_like(m_ref, NEG_BIG)
        l_ref[...] = jnp.zeros_like(l_ref)
        run_ref[...] = jnp.zeros_like(run_ref)
        acc_ref[...] = jnp.zeros_like(acc_ref)
        accsb_ref[...] = jnp.zeros_like(accsb_ref)
        q_t = lax.broadcasted_iota(jnp.int32, (rows, 1), 0) & (t_new - 1)
        k_t = lax.broadcasted_iota(jnp.int32, (1, page), 1)
        update(newk_ref[:, 0:c], newk_ref[:, c:c + 64], newsb_ref[:, 0:LANE], newsb_ref[:, LANE:2 * LANE], 1,
               (k_t <= q_t) & (k_t < t_new), (k_t < q_t) & (k_t < t_new))

    for j in range(pps):
        kbuf[j * page:(j + 1) * page, :] = mla_pages[j][...].astype(BF16)
        sbuf[j * page:(j + 1) * page, :] = sb_pages[j][...].astype(BF16)
    update(kbuf[:, 0:c], kbuf[:, c:c + 64], sbuf[:, 0:LANE], sbuf[:, LANE:2 * LANE], pps, None, None)

    @pl.when(ch == pl.num_programs(1) - 1)
    def _():
        olat_ref[...] = acc_ref[...] / l_ref[...]
        osb_ref[...] = accsb_ref[...]


def _sample_attention(page_table, layer, q_lat, q_rope, sq, newk, newsb, cache_mla, cache_sb,
                      t_new, c, mla_scale, sb_scale):
    nb, rows, _ = q_lat.shape
    n_pages = page_table.shape[1]
    pps = _pick(n_pages, (16, 8, 4, 2, 1))
    page = cache_mla.shape[2]
    assert page == LANE and t_new & (t_new - 1) == 0

    def page_spec(width, j):
        return pl.BlockSpec((None, None, page, width),
                            lambda b, ch, pt: (layer, pt[b * n_pages + n_pages - 1 - (ch * pps + j)], 0, 0))

    per_b = lambda r, w: pl.BlockSpec((None, r, w), lambda b, ch, pt: (b, 0, 0))
    grid_spec = pltpu.PrefetchScalarGridSpec(
        num_scalar_prefetch=1,
        grid=(nb, n_pages // pps),
        in_specs=[per_b(rows, c), per_b(rows, LANE), per_b(rows, LANE), per_b(page, c + 64), per_b(page, 2 * LANE)]
                 + [page_spec(c + 64, j) for j in range(pps)] + [page_spec(2 * LANE, j) for j in range(pps)],
        out_specs=[per_b(rows, c), per_b(rows, LANE)],
        scratch_shapes=[pltpu.VMEM((pps * page, c + 64), BF16), pltpu.VMEM((pps * page, 2 * LANE), BF16),
                        pltpu.VMEM((rows, 1), F32), pltpu.VMEM((rows, 1), F32), pltpu.VMEM((rows, 1), F32),
                        pltpu.VMEM((rows, c), F32), pltpu.VMEM((rows, LANE), F32)])
    return pl.pallas_call(
        functools.partial(_sample_attn_kernel, pps=pps, c=c, t_new=t_new, mla_scale=mla_scale, sb_scale=sb_scale),
        grid_spec=grid_spec,
        out_shape=[jax.ShapeDtypeStruct((nb, rows, c), F32), jax.ShapeDtypeStruct((nb, rows, LANE), F32)],
        compiler_params=_cparams(2), name="sample_attn")(
            page_table.reshape(-1), q_lat, q_rope, sq, newk, newsb,
            *([cache_mla] * pps), *([cache_sb] * pps))


def _uvproj_kernel(o_ref, w_ref, out_ref):
    out_ref[...] = _dot(o_ref[...], w_ref[...]).astype(out_ref.dtype)


def _uvproj(o_lat_h, w_uvh):
    heads, T, c = o_lat_h.shape
    v = w_uvh.shape[2]
    return pl.pallas_call(
        _uvproj_kernel,
        grid=(heads,),
        in_specs=[pl.BlockSpec((None, T, c), lambda h: (h, 0, 0)), pl.BlockSpec((None, c, v), lambda h: (h, 0, 0))],
        out_specs=pl.BlockSpec((T, v), lambda h: (0, h)),
        out_shape=jax.ShapeDtypeStruct((T, heads * v), BF16),
        compiler_params=_cparams(1), name="uvproj")(o_lat_h, w_uvh)


def _merge_kernel(oa_ref, ob_ref, wpa_ref, wpb_ref, ga_ref, gb_ref, o_ref):
    o_a = _dot(oa_ref[...], wpa_ref[...])
    o_b = _dot(ob_ref[...], wpb_ref[...])
    o_ref[...] = (jax.nn.sigmoid(ga_ref[...]) * o_a + jax.nn.sigmoid(gb_ref[...]) * o_b).astype(o_ref.dtype)


def _merge(oa, ob, w_pa, w_pb, z_g):
    M, ka = oa.shape
    kb = ob.shape[1]
    D = w_pa.shape[1]
    bm = _pick(M, (512, 256, 128, 64, 32, 16, 8))
    bn = _pick(D, (1024, 512, 256, 128))
    nb = D // bn
    return pl.pallas_call(
        _merge_kernel,
        grid=(nb, M // bm),
        in_specs=[pl.BlockSpec((bm, ka), lambda n, m: (m, 0)), pl.BlockSpec((bm, kb), lambda n, m: (m, 0)),
                  pl.BlockSpec((ka, bn), lambda n, m: (0, n)), pl.BlockSpec((kb, bn), lambda n, m: (0, n)),
                  pl.BlockSpec((bm, bn), lambda n, m: (m, n)), pl.BlockSpec((bm, bn), lambda n, m: (m, n + nb))],
        out_specs=pl.BlockSpec((bm, bn), lambda n, m: (m, n)),
        out_shape=jax.ShapeDtypeStruct((M, D), BF16),
        compiler_params=_cparams(2), name="merge")(oa, ob, w_pa, w_pb, z_g, z_g)


def _ln_kernel(x_ref, y_ref, g_ref, b_ref, o_ref, ob_ref, *, alpha):
    v = alpha * x_ref[...] + y_ref[...]
    mu = jnp.mean(v, axis=-1, keepdims=True)
    d = v - mu
    var = jnp.mean(jnp.square(d), axis=-1, keepdims=True)
    out = d * lax.rsqrt(var + LN_EPS) * g_ref[...] + b_ref[...]
    o_ref[...] = out
    ob_ref[...] = out.astype(BF16)


def _ln_residual(x, y, g, b, alpha):
    M, D = x.shape
    tm = _pick(M, (128, 64, 32, 16, 8))
    row = pl.BlockSpec((tm, D), lambda i: (i, 0))
    vec = pl.BlockSpec((1, D), lambda i: (0, 0))
    return pl.pallas_call(
        functools.partial(_ln_kernel, alpha=alpha),
        grid=(M // tm,),
        in_specs=[row, row, vec, vec],
        out_specs=[row, row],
        out_shape=[jax.ShapeDtypeStruct((M, D), F32), jax.ShapeDtypeStruct((M, D), BF16)],
        compiler_params=_cparams(1), name="ln_residual")(x, y, g, b)


def _rope_tables(pos):
    inv_freq = 1.0 / (ROPE_THETA ** (jnp.arange(0, 64, 2, dtype=F32) / 64))
    ang = pos.astype(F32)[:, None] * inv_freq[None, :]
    cos, sin = jnp.cos(ang), jnp.sin(ang)
    zeros = jnp.zeros((pos.shape[0], 64), F32)
    return jnp.concatenate([cos, cos, zeros], axis=1), jnp.concatenate([-sin, sin, zeros], axis=1)


def kernel(x_prompt, x_sample, cache_mla, cache_sb, page_table, w_in, q_norm_g, w_uq, kv_norm_g, w_uk, w_uv, w_pa,
           w_pb, w_o, ln1_g, ln1_b, w_up, w_down, ln2_g, ln2_b):
    depth, D, _ = w_in.shape
    B, S, _ = x_prompt.shape
    NB, T, _ = x_sample.shape
    ql = q_norm_g.shape[1]
    c = kv_norm_g.shape[1]
    heads, nope = w_uk.shape[2], w_uk.shape[3]
    rope_d = w_uq.shape[3] - nope
    sbd = cache_sb.shape[-1] // 2
    sb_heads = w_pb.shape[1] // sbd
    assert (c, rope_d, sbd, nope, w_uv.shape[3]) == (256, 64, LANE, LANE, LANE) and sb_heads == heads
    assert S % K_TILE == 0 and cache_mla.shape[-1] == c + rope_d
    past_len = page_table.shape[1] * cache_mla.shape[2]
    alpha = (2 * depth) ** 0.25
    mla_scale = (nope + rope_d) ** -0.5
    sb_scale = sbd ** -0.5
    mp, ms = B * S, NB * T
    M = mp + ms

    o_ckv, o_kr, o_sq = ql, ql + c, ql + c + rope_d
    o_sk = o_sq + heads * sbd
    o_ga = o_sk + 2 * sbd
    wa_cols = ql + c + 3 * LANE
    wa_pad = -wa_cols % 512

    pos = jnp.concatenate([jnp.tile(jnp.arange(S, dtype=jnp.int32), B),
                           jnp.tile(past_len + jnp.arange(T, dtype=jnp.int32), NB)])
    cos_t, sin_t = _rope_tables(pos)

    x = jnp.concatenate([x_prompt.reshape(mp, D), x_sample.reshape(ms, D)], axis=0)
    xb = x.astype(BF16)
    outs = [[], [], [], []]
    for l in range(depth):
        wl = w_in[l]
        w_a = jnp.concatenate([wl[:, :o_sq], jnp.zeros((D, LANE - rope_d), F32), wl[:, o_sk:o_ga],
                               jnp.zeros((D, wa_pad), F32)], axis=1).astype(BF16)
        w_sq = wl[:, o_sq:o_sk].astype(BF16)
        w_g = wl[:, o_ga:].astype(BF16)
        w_q = jnp.concatenate([w_uq[l][:, :, :nope].reshape(ql, heads * nope),
                               jnp.pad(w_uq[l][:, :, nope:], ((0, 0), (0, 0), (0, LANE - rope_d))).reshape(ql, heads * LANE)],
                              axis=1).astype(BF16)
        w_ukt = w_uk[l].transpose(1, 2, 0).astype(BF16)
        w_uvh = w_uv[l].transpose(1, 0, 2).astype(BF16)

        z_a = _mm(xb, w_a, F32, name="in_proj_a")
        sq = _mm(xb, w_sq, BF16, name="in_proj_sq")
        z_g = _mm(xb, w_g, F32, name="in_proj_gates")
        cqn, mla_rows, kmla, sb_rows, ksb = _rows(z_a, q_norm_g[l][None], kv_norm_g[l][None], cos_t, sin_t, ql, c)
        q = _mm(cqn, w_q, F32, name="q_up")
        q_lat, q_rope = _qbuild(q, w_ukt, cos_t, sin_t, heads, nope, c)

        oa_p, osb_p = _prompt_attention(q_lat, q_rope, sq, kmla, ksb, w_uvh, B, S, heads, c, mla_scale, sb_scale)

        def per_batch(a):
            w = a.shape[1] // heads
            return a[mp:].reshape(NB, T, heads, w).transpose(0, 2, 1, 3).reshape(NB, heads * T, w)

        def new_keys(a):
            return jnp.pad(a[mp:].reshape(NB, T, a.shape[1]), ((0, 0), (0, LANE - T), (0, 0)))

        o_lat_s, o_sb_s = _sample_attention(page_table, l, per_batch(q_lat), per_batch(q_rope), per_batch(sq),
                                            new_keys(kmla), new_keys(ksb), cache_mla, cache_sb,
                                            T, c, mla_scale, sb_scale)
        o_lat_h = o_lat_s.reshape(NB, heads, T, c).transpose(1, 0, 2, 3).reshape(heads, ms, c).astype(BF16)
        oa_s = _uvproj(o_lat_h, w_uvh)
        osb_s = o_sb_s.reshape(NB, heads, T, sbd).transpose(0, 2, 1, 3).reshape(ms, heads * sbd).astype(BF16)

        merged = _merge(jnp.concatenate([oa_p, oa_s], axis=0), jnp.concatenate([osb_p, osb_s], axis=0),
                        w_pa[l].astype(BF16), w_pb[l].astype(BF16), z_g)
        mix = _mm(merged, w_o[l].astype(BF16), F32, name="out_proj")
        h, hb = _ln_residual(x, mix, ln1_g[l][None], ln1_b[l][None], alpha)
        up = _mm(hb, w_up[l].astype(BF16), BF16, act="relu2", name="ffn_up")
        f = _mm(up, w_down[l].astype(BF16), F32, name="ffn_down")
        x, xb = _ln_residual(h, f, ln2_g[l][None], ln2_b[l][None], alpha)

        outs[0].append(mla_rows[:mp].reshape(B, S, c + rope_d))
        outs[1].append(sb_rows[:mp].reshape(B, S, 2 * sbd))
        outs[2].append(mla_rows[mp:].reshape(NB, T, c + rope_d))
        outs[3].append(sb_rows[mp:].reshape(NB, T, 2 * sbd))

    return (x[:mp].reshape(B, S, D), x[mp:].reshape(NB, T, D),
            jnp.stack(outs[0]), jnp.stack(outs[1]), jnp.stack(outs[2]), jnp.stack(outs[3]))
```

```python
import functools

import jax
import jax.numpy as jnp
from jax import lax
from jax.experimental import pallas as pl
from jax.experimental.pallas import tpu as pltpu

F32 = jnp.float32
BF16 = jnp.bfloat16

ROPE_THETA = 10000.0
LN_EPS = 1e-5
RMS_EPS = 1e-6
NEG_BIG = -1e30

LANE = 128
Q_TILE = 128
K_TILE = 256
HEAD_GROUP = 16
PAGES_PER_STEP = 32
VMEM_LIMIT = 56 * 1024 * 1024


def _cparams(n_axes):
    return pltpu.CompilerParams(dimension_semantics=("arbitrary",) * n_axes, vmem_limit_bytes=VMEM_LIMIT)


def _pick(n, candidates):
    for c in candidates:
        if n % c == 0:
            return c
    raise ValueError(f"no block size among {candidates} divides {n}")


def _dot(a, b):
    return jnp.dot(a, b, preferred_element_type=F32)


def _dot_nt(a, b):
    return lax.dot_general(a, b, (((1,), (1,)), ((), ())), preferred_element_type=F32)


def _act(y, act):
    return jnp.square(jnp.maximum(y, 0.0)) if act == "relu2" else y


def _mm_kernel(x_ref, w_ref, o_ref, *, act):
    o_ref[...] = _act(_dot(x_ref[...], w_ref[...]), act).astype(o_ref.dtype)


def _mm_cast_kernel(x_ref, w_ref, o_ref, wb_ref, *, act):
    @pl.when(pl.program_id(1) == 0)
    def _():
        wb_ref[...] = w_ref[...].astype(BF16)

    o_ref[...] = _act(_dot(x_ref[...], wb_ref[...]), act).astype(o_ref.dtype)


def _mm_acc_kernel(x_ref, w_ref, o_ref, acc_ref):
    k = pl.program_id(2)

    @pl.when(k == 0)
    def _():
        acc_ref[...] = jnp.zeros_like(acc_ref)

    acc_ref[...] += _dot(x_ref[...], w_ref[...])

    @pl.when(k == pl.num_programs(2) - 1)
    def _():
        o_ref[...] = acc_ref[...].astype(o_ref.dtype)


def _mm(x, w, out_dtype, act=None, layer=None, name="mm"):
    M, K = x.shape
    N = w.shape[-1]
    cast = w.dtype == F32
    bm = _pick(M, (512, 256, 128, 64, 32, 16, 8))
    bn = _pick(N, (512, 256, 128) if cast else (1024, 512, 256, 128))
    bk = 4096
    wdims = (K, bn) if layer is None else (None, K, bn)
    if K <= bk:
        widx = (lambda n, m: (0, n)) if layer is None else (lambda n, m: (layer, 0, n))
        return pl.pallas_call(
            functools.partial(_mm_cast_kernel if cast else _mm_kernel, act=act),
            grid=(N // bn, M // bm),
            in_specs=[pl.BlockSpec((bm, K), lambda n, m: (m, 0)), pl.BlockSpec(wdims, widx)],
            out_specs=pl.BlockSpec((bm, bn), lambda n, m: (m, n)),
            out_shape=jax.ShapeDtypeStruct((M, N), out_dtype),
            scratch_shapes=[pltpu.VMEM((K, bn), BF16)] if cast else [],
            compiler_params=_cparams(2), name=name)(x, w)
    assert act is None and K % bk == 0 and not cast
    wdims = (bk, bn) if layer is None else (None, bk, bn)
    widx = (lambda n, m, k: (k, n)) if layer is None else (lambda n, m, k: (layer, k, n))
    return pl.pallas_call(
        _mm_acc_kernel,
        grid=(N // bn, M // bm, K // bk),
        in_specs=[pl.BlockSpec((bm, bk), lambda n, m, k: (m, k)), pl.BlockSpec(wdims, widx)],
        out_specs=pl.BlockSpec((bm, bn), lambda n, m, k: (m, n)),
        out_shape=jax.ShapeDtypeStruct((M, N), out_dtype),
        scratch_shapes=[pltpu.VMEM((bm, bn), F32)],
        compiler_params=_cparams(3), name=name)(x, w)


def _rope_lanes(x, cos, sin):
    lane = lax.broadcasted_iota(jnp.int32, x.shape, 1)
    swapped = jnp.where(lane < 32, pltpu.roll(x, 96, 1), pltpu.roll(x, 32, 1))
    return x * cos + swapped * sin


def _rows_kernel(z_ref, gq_ref, gkv_ref, cos_ref, sin_ref,
                 cqn_ref, mla_ref, kmla_ref, sb_ref, ksb_ref, *, ql, c):
    cq = z_ref[:, :ql]
    cqn = cq * lax.rsqrt(jnp.mean(jnp.square(cq), axis=-1, keepdims=True) + RMS_EPS) * gq_ref[...]
    cqn_ref[...] = cqn.astype(BF16)
    ckv = z_ref[:, ql:ql + c]
    ckvn = ckv * lax.rsqrt(jnp.mean(jnp.square(ckv), axis=-1, keepdims=True) + RMS_EPS) * gkv_ref[...]
    kr = _rope_lanes(z_ref[:, ql + c:ql + c + LANE], cos_ref[...], sin_ref[...])[:, :64]
    mla_ref[:, :c] = ckvn
    mla_ref[:, c:] = kr
    kmla_ref[:, :c] = ckvn.astype(BF16)
    kmla_ref[:, c:] = kr.astype(BF16)
    sb = z_ref[:, ql + c + LANE:ql + c + 3 * LANE]
    sb_ref[...] = sb
    ksb_ref[...] = sb.astype(BF16)


def _rows(z_a, gq, gkv, cos_t, sin_t, ql, c):
    M, wa = z_a.shape
    tm = _pick(M, (256, 128, 64, 32, 16, 8))
    row = lambda w: pl.BlockSpec((tm, w), lambda i: (i, 0))
    full = lambda w: pl.BlockSpec((1, w), lambda i: (0, 0))
    return pl.pallas_call(
        functools.partial(_rows_kernel, ql=ql, c=c),
        grid=(M // tm,),
        in_specs=[row(wa), full(ql), full(c), row(LANE), row(LANE)],
        out_specs=[row(ql), row(c + 64), row(c + 64), row(2 * LANE), row(2 * LANE)],
        out_shape=[jax.ShapeDtypeStruct((M, ql), BF16),
                   jax.ShapeDtypeStruct((M, c + 64), F32),
                   jax.ShapeDtypeStruct((M, c + 64), BF16),
                   jax.ShapeDtypeStruct((M, 2 * LANE), F32),
                   jax.ShapeDtypeStruct((M, 2 * LANE), BF16)],
        compiler_params=_cparams(1), name="rows")(z_a, gq, gkv, cos_t, sin_t)


def _qbuild_kernel(q_ref, wuk_ref, cos_ref, sin_ref, qlat_ref, qrope_ref, *, heads, nope, c):
    cos = cos_ref[...]
    sin = sin_ref[...]
    for h in range(heads):
        qn = q_ref[:, h * nope:(h + 1) * nope].astype(BF16)
        qlat_ref[:, h * c:(h + 1) * c] = _dot(qn, wuk_ref[h]).astype(BF16)
        qr = q_ref[:, heads * nope + h * LANE:heads * nope + (h + 1) * LANE]
        qrope_ref[:, h * LANE:(h + 1) * LANE] = _rope_lanes(qr, cos, sin).astype(BF16)


def _qbuild(q, w_ukt, cos_t, sin_t, heads, nope, c):
    M = q.shape[0]
    tm = _pick(M, (256, 128, 64, 32, 16, 8))
    row = lambda w: pl.BlockSpec((tm, w), lambda i: (i, 0))
    return pl.pallas_call(
        functools.partial(_qbuild_kernel, heads=heads, nope=nope, c=c),
        grid=(M // tm,),
        in_specs=[row(q.shape[1]), pl.BlockSpec(w_ukt.shape, lambda i: (0, 0, 0)), row(LANE), row(LANE)],
        out_specs=[row(heads * c), row(heads * LANE)],
        out_shape=[jax.ShapeDtypeStruct((M, heads * c), BF16),
                   jax.ShapeDtypeStruct((M, heads * LANE), BF16)],
        compiler_params=_cparams(1), name="qbuild")(q, w_ukt, cos_t, sin_t)


def _suffix_matrix(n):
    j = lax.broadcasted_iota(jnp.int32, (n, n), 0)
    s = lax.broadcasted_iota(jnp.int32, (n, n), 1)
    return (j > s).astype(BF16)


def _suffix_sum(x, u):
    hi = x.astype(BF16)
    lo = (x - hi.astype(F32)).astype(BF16)
    return _dot(hi, u) + _dot(lo, u)


def _sb_logs(z, mask):
    t = jnp.log(1.0 + jnp.exp(-jnp.abs(z)))
    log_beta = jnp.minimum(z, 0.0) - t
    log_keep = -jnp.maximum(z, 0.0) - t
    if mask is not None:
        log_keep = jnp.where(mask, log_keep, 0.0)
    return log_beta, log_keep


def _prompt_attn_kernel(qlat_ref, qrope_ref, sq_ref, kmla_ref, ksb_ref, wuv_ref, oa_ref, osb_ref,
                        acc_ref, accsb_ref, *, hg, c, mla_scale, sb_scale):
    i = pl.program_id(1)
    tq, tk = Q_TILE, K_TILE
    mq = hg * tq
    q_lat = jnp.concatenate([qlat_ref[:, h * c:(h + 1) * c] for h in range(hg)], axis=0)
    q_rope = jnp.concatenate([qrope_ref[:, h * LANE:h * LANE + 64] for h in range(hg)], axis=0)
    q_sb = jnp.concatenate([sq_ref[:, h * LANE:(h + 1) * LANE] for h in range(hg)], axis=0)
    q_pos = i * tq + (lax.broadcasted_iota(jnp.int32, (mq, 1), 0) & (tq - 1))
    n_kb = (i * tq + tq - 1) // tk + 1
    u = _suffix_matrix(tk)
    acc_ref[...] = jnp.zeros_like(acc_ref)
    accsb_ref[...] = jnp.zeros_like(accsb_ref)

    def block(kb, masked, carry):
        m, l, run = carry
        start = pl.multiple_of(kb * tk, tk)
        s = (_dot_nt(q_lat, kmla_ref[pl.ds(start, tk), 0:c])
             + _dot_nt(q_rope, kmla_ref[pl.ds(start, tk), c:c + 64])) * mla_scale
        mask = None
        if masked:
            k_pos = start + lax.broadcasted_iota(jnp.int32, (1, tk), 1)
            s = jnp.where(k_pos <= q_pos, s, NEG_BIG)
            mask = k_pos < q_pos
        m_new = jnp.maximum(m, jnp.max(s, axis=-1, keepdims=True))
        alpha = jnp.exp(m - m_new)
        p = jnp.exp(s - m_new)
        l = alpha * l + jnp.sum(p, axis=-1, keepdims=True)
        acc_ref[...] = alpha * acc_ref[...] + _dot(p.astype(BF16), kmla_ref[pl.ds(start, tk), 0:c])
        z = _dot_nt(q_sb, ksb_ref[pl.ds(start, tk), 0:LANE]) * sb_scale
        log_beta, log_keep = _sb_logs(z, mask)
        a = jnp.exp(log_beta + _suffix_sum(log_keep, u) + run)
        if masked:
            a = jnp.where(mask, a, 0.0)
        accsb_ref[...] += _dot(a.astype(BF16), ksb_ref[pl.ds(start, tk), LANE:2 * LANE])
        run = run + jnp.sum(log_keep, axis=-1, keepdims=True)
        return m_new, l, run

    init = (jnp.full((mq, 1), NEG_BIG, F32), jnp.zeros((mq, 1), F32), jnp.zeros((mq, 1), F32))
    carry = block(n_kb - 1, True, init)
    _, l, _ = lax.fori_loop(0, n_kb - 1, lambda it, cr: block(n_kb - 2 - it, False, cr), carry)
    o_lat = (acc_ref[...] / l).astype(BF16)
    for h in range(hg):
        oa_ref[:, h * LANE:(h + 1) * LANE] = _dot(o_lat[h * tq:(h + 1) * tq], wuv_ref[h]).astype(BF16)
        osb_ref[:, h * LANE:(h + 1) * LANE] = accsb_ref[h * tq:(h + 1) * tq, :].astype(BF16)


def _prompt_attention(q_lat, q_rope, sq, kmla, ksb, w_uvh, batch, seq, heads, c, mla_scale, sb_scale):
    hg = _pick(heads, (HEAD_GROUP, 4, 1))
    nq = seq // Q_TILE
    assert seq % K_TILE == 0
    qspec = lambda w: pl.BlockSpec((Q_TILE, hg * w), lambda b, i, g: (b * nq + i, g))
    kspec = lambda w: pl.BlockSpec((seq, w), lambda b, i, g: (b, 0))
    return pl.pallas_call(
        functools.partial(_prompt_attn_kernel, hg=hg, c=c, mla_scale=mla_scale, sb_scale=sb_scale),
        grid=(batch, nq, heads // hg),
        in_specs=[qspec(c), qspec(LANE), qspec(LANE), kspec(c + 64), kspec(2 * LANE),
                  pl.BlockSpec((hg, c, LANE), lambda b, i, g: (g, 0, 0))],
        out_specs=[qspec(LANE), qspec(LANE)],
        out_shape=[jax.ShapeDtypeStruct((batch * seq, heads * LANE), BF16)] * 2,
        scratch_shapes=[pltpu.VMEM((hg * Q_TILE, c), F32), pltpu.VMEM((hg * Q_TILE, LANE), F32)],
        compiler_params=_cparams(3), name="prompt_attn")(q_lat, q_rope, sq, kmla, ksb, w_uvh)


def _sample_attn_kernel(pt_ref, qlat_ref, qrope_ref, sq_ref, newkt_ref, newsb_ref, *rest,
                        pps, c, t_new, mla_scale, sb_scale):
    del pt_ref
    mla_pages = rest[:pps]
    sb_pages = rest[pps:2 * pps]
    olat_ref, osb_ref = rest[2 * pps:2 * pps + 2]
    kbuf, sbuf, m_ref, l_ref, run_ref, acc_ref, accsb_ref = rest[2 * pps + 2:]
    ch = pl.program_id(1)
    page = LANE
    q_lat = qlat_ref[...]
    q_rope = qrope_ref[:, 0:64]
    q_sb = sq_ref[...]
    rows = q_lat.shape[0]
    u = _suffix_matrix(page)

    def update(kt_lat, kt_rope, sb_k, sb_v, n, mla_mask, sb_mask):
        s = (_dot(q_lat, kt_lat) + _dot(q_rope, kt_rope)) * mla_scale
        if mla_mask is not None:
            s = jnp.where(mla_mask, s, NEG_BIG)
        m = m_ref[...]
        m_new = jnp.maximum(m, jnp.max(s, axis=-1, keepdims=True))
        alpha = jnp.exp(m - m_new)
        p = jnp.exp(s - m_new)
        l_ref[...] = alpha * l_ref[...] + jnp.sum(p, axis=-1, keepdims=True)
        m_ref[...] = m_new
        acc_ref[...] = alpha * acc_ref[...] + _dot_nt(p.astype(BF16), kt_lat)
        z = jnp.concatenate([_dot_nt(q_sb, sb_k[j * page:(j + 1) * page]) for j in range(n)], axis=0) * sb_scale
        log_beta, log_keep = _sb_logs(z, sb_mask)
        within = _suffix_sum(log_keep, u)
        tot = jnp.sum(log_keep, axis=-1, keepdims=True)
        run = run_ref[...]
        runs = []
        for j in range(n):
            runs.append(run)
            run = run + tot[j * rows:(j + 1) * rows]
        run_ref[...] = run
        a = jnp.exp(log_beta + within + jnp.concatenate(runs, axis=0))
        if sb_mask is not None:
            a = jnp.where(sb_mask, a, 0.0)
        a = a.astype(BF16)
        upd = _dot(a[0:rows], sb_v[0:page])
        for j in range(1, n):
            upd += _dot(a[j * rows:(j + 1) * rows], sb_v[j * page:(j + 1) * page])
        accsb_ref[...] += upd

    @pl.when(ch == 0)
    def _():
        m_ref[...] = jnp.full_like(m_ref, NEG_BIG)
        l_ref[...] = jnp.zeros_like(l_ref)
        run_ref[...] = jnp.zeros_like(run_ref)
        acc_ref[...] = jnp.zeros_like(acc_ref)
        accsb_ref[...] = jnp.zeros_like(accsb_ref)
        q_t = lax.broadcasted_iota(jnp.int32, (rows, 1), 0) & (t_new - 1)
        k_t = lax.broadcasted_iota(jnp.int32, (1, page), 1)
        update(newkt_ref[0:c, :], newkt_ref[c:c + 64, :], newsb_ref[:, 0:LANE], newsb_ref[:, LANE:2 * LANE], 1,
               (k_t <= q_t) & (k_t < t_new), (k_t < q_t) & (k_t < t_new))

    for j in range(pps):
        kbuf[:, j * page:(j + 1) * page] = mla_pages[j][...].astype(BF16)
        sbuf[j * page:(j + 1) * page, :] = sb_pages[j][...].astype(BF16)
    update(kbuf[0:c, :], kbuf[c:c + 64, :], sbuf[:, 0:LANE], sbuf[:, LANE:2 * LANE], pps, None, None)

    @pl.when(ch == pl.num_programs(1) - 1)
    def _():
        olat_ref[...] = acc_ref[...] / l_ref[...]
        osb_ref[...] = accsb_ref[...]


def _sample_attention(page_table, layer, q_lat, q_rope, sq, newk_t, newsb, cache_mla_t, cache_sb,
                      t_new, c, mla_scale, sb_scale):
    nb, rows, _ = q_lat.shape
    n_pages = page_table.shape[1]
    pps = _pick(n_pages, (PAGES_PER_STEP, 16, 8, 4, 2, 1))
    page = cache_sb.shape[2]
    assert page == LANE and t_new & (t_new - 1) == 0

    def page_spec(r, w, j):
        return pl.BlockSpec((None, None, r, w),
                            lambda b, ch, pt: (layer, pt[b * n_pages + n_pages - 1 - (ch * pps + j)], 0, 0))

    per_b = lambda r, w: pl.BlockSpec((None, r, w), lambda b, ch, pt: (b, 0, 0))
    grid_spec = pltpu.PrefetchScalarGridSpec(
        num_scalar_prefetch=1,
        grid=(nb, n_pages // pps),
        in_specs=[per_b(rows, c), per_b(rows, LANE), per_b(rows, LANE), per_b(c + 64, page), per_b(page, 2 * LANE)]
                 + [page_spec(c + 64, page, j) for j in range(pps)] + [page_spec(page, 2 * LANE, j) for j in range(pps)],
        out_specs=[per_b(rows, c), per_b(rows, LANE)],
        scratch_shapes=[pltpu.VMEM((c + 64, pps * page), BF16), pltpu.VMEM((pps * page, 2 * LANE), BF16),
                        pltpu.VMEM((rows, 1), F32), pltpu.VMEM((rows, 1), F32), pltpu.VMEM((rows, 1), F32),
                        pltpu.VMEM((rows, c), F32), pltpu.VMEM((rows, LANE), F32)])
    return pl.pallas_call(
        functools.partial(_sample_attn_kernel, pps=pps, c=c, t_new=t_new, mla_scale=mla_scale, sb_scale=sb_scale),
        grid_spec=grid_spec,
        out_shape=[jax.ShapeDtypeStruct((nb, rows, c), F32), jax.ShapeDtypeStruct((nb, rows, LANE), F32)],
        compiler_params=_cparams(2), name="sample_attn")(
            page_table.reshape(-1), q_lat, q_rope, sq, newk_t, newsb,
            *([cache_mla_t] * pps), *([cache_sb] * pps))


def _uvproj_kernel(o_ref, w_ref, out_ref):
    out_ref[...] = _dot(o_ref[...], w_ref[...]).astype(out_ref.dtype)


def _uvproj(o_lat_h, w_uvh):
    heads, T, c = o_lat_h.shape
    v = w_uvh.shape[2]
    return pl.pallas_call(
        _uvproj_kernel,
        grid=(heads,),
        in_specs=[pl.BlockSpec((None, T, c), lambda h: (h, 0, 0)), pl.BlockSpec((None, c, v), lambda h: (h, 0, 0))],
        out_specs=pl.BlockSpec((T, v), lambda h: (0, h)),
        out_shape=jax.ShapeDtypeStruct((T, heads * v), BF16),
        compiler_params=_cparams(1), name="uvproj")(o_lat_h, w_uvh)


def _merge_kernel(oa_ref, ob_ref, wpa_ref, wpb_ref, ga_ref, gb_ref, o_ref, wa_ref, wb_ref):
    @pl.when(pl.program_id(1) == 0)
    def _():
        wa_ref[...] = wpa_ref[...].astype(BF16)
        wb_ref[...] = wpb_ref[...].astype(BF16)

    o_a = _dot(oa_ref[...], wa_ref[...])
    o_b = _dot(ob_ref[...], wb_ref[...])
    o_ref[...] = (jax.nn.sigmoid(ga_ref[...]) * o_a + jax.nn.sigmoid(gb_ref[...]) * o_b).astype(o_ref.dtype)


def _merge(oa, ob, w_pa, w_pb, layer, z_g):
    M, ka = oa.shape
    kb = ob.shape[1]
    D = w_pa.shape[2]
    bm = _pick(M, (512, 256, 128, 64, 32, 16, 8))
    bn = _pick(D, (512, 256, 128))
    nb = D // bn
    return pl.pallas_call(
        _merge_kernel,
        grid=(nb, M // bm),
        in_specs=[pl.BlockSpec((bm, ka), lambda n, m: (m, 0)), pl.BlockSpec((bm, kb), lambda n, m: (m, 0)),
                  pl.BlockSpec((None, ka, bn), lambda n, m: (layer, 0, n)),
                  pl.BlockSpec((None, kb, bn), lambda n, m: (layer, 0, n)),
                  pl.BlockSpec((bm, bn), lambda n, m: (m, n)), pl.BlockSpec((bm, bn), lambda n, m: (m, n + nb))],
        out_specs=pl.BlockSpec((bm, bn), lambda n, m: (m, n)),
        out_shape=jax.ShapeDtypeStruct((M, D), BF16),
        scratch_shapes=[pltpu.VMEM((ka, bn), BF16), pltpu.VMEM((kb, bn), BF16)],
        compiler_params=_cparams(2), name="merge")(oa, ob, w_pa, w_pb, z_g, z_g)


def _ln_kernel(x_ref, y_ref, g_ref, b_ref, o_ref, ob_ref, *, alpha):
    v = alpha * x_ref[...] + y_ref[...]
    mu = jnp.mean(v, axis=-1, keepdims=True)
    d = v - mu
    var = jnp.mean(jnp.square(d), axis=-1, keepdims=True)
    out = d * lax.rsqrt(var + LN_EPS) * g_ref[...] + b_ref[...]
    o_ref[...] = out
    ob_ref[...] = out.astype(BF16)


def _ln_residual(x, y, g, b, alpha):
    M, D = x.shape
    tm = _pick(M, (128, 64, 32, 16, 8))
    row = pl.BlockSpec((tm, D), lambda i: (i, 0))
    vec = pl.BlockSpec((1, D), lambda i: (0, 0))
    return pl.pallas_call(
        functools.partial(_ln_kernel, alpha=alpha),
        grid=(M // tm,),
        in_specs=[row, row, vec, vec],
        out_specs=[row, row],
        out_shape=[jax.ShapeDtypeStruct((M, D), F32), jax.ShapeDtypeStruct((M, D), BF16)],
        compiler_params=_cparams(1), name="ln_residual")(x, y, g, b)


def _rope_tables(pos):
    inv_freq = 1.0 / (ROPE_THETA ** (jnp.arange(0, 64, 2, dtype=F32) / 64))
    ang = pos.astype(F32)[:, None] * inv_freq[None, :]
    cos, sin = jnp.cos(ang), jnp.sin(ang)
    zeros = jnp.zeros((pos.shape[0], 64), F32)
    return jnp.concatenate([cos, cos, zeros], axis=1), jnp.concatenate([-sin, sin, zeros], axis=1)


def kernel(x_prompt, x_sample, cache_mla, cache_sb, page_table, w_in, q_norm_g, w_uq, kv_norm_g, w_uk, w_uv, w_pa,
           w_pb, w_o, ln1_g, ln1_b, w_up, w_down, ln2_g, ln2_b):
    depth, D, _ = w_in.shape
    B, S, _ = x_prompt.shape
    NB, T, _ = x_sample.shape
    ql = q_norm_g.shape[1]
    c = kv_norm_g.shape[1]
    heads, nope = w_uk.shape[2], w_uk.shape[3]
    rope_d = w_uq.shape[3] - nope
    sbd = cache_sb.shape[-1] // 2
    sb_heads = w_pb.shape[1] // sbd
    assert (c, rope_d, sbd, nope, w_uv.shape[3]) == (256, 64, LANE, LANE, LANE) and sb_heads == heads
    assert S % K_TILE == 0 and cache_mla.shape[-1] == c + rope_d
    past_len = page_table.shape[1] * cache_mla.shape[2]
    alpha = (2 * depth) ** 0.25
    mla_scale = (nope + rope_d) ** -0.5
    sb_scale = sbd ** -0.5
    mp, ms = B * S, NB * T
    M = mp + ms

    o_ckv, o_kr, o_sq = ql, ql + c, ql + c + rope_d
    o_sk = o_sq + heads * sbd
    o_ga = o_sk + 2 * sbd
    wa_cols = ql + c + 3 * LANE
    wa_pad = -wa_cols % 512

    pos = jnp.concatenate([jnp.tile(jnp.arange(S, dtype=jnp.int32), B),
                           jnp.tile(past_len + jnp.arange(T, dtype=jnp.int32), NB)])
    cos_t, sin_t = _rope_tables(pos)

    cache_mla_t = jnp.swapaxes(cache_mla, 2, 3)
    w_down_b = w_down.astype(BF16)

    x = jnp.concatenate([x_prompt.reshape(mp, D), x_sample.reshape(ms, D)], axis=0)
    xb = x.astype(BF16)
    outs = [[], [], [], []]
    for l in range(depth):
        wl = w_in[l]
        w_a = jnp.concatenate([wl[:, :o_sq], jnp.zeros((D, LANE - rope_d), F32), wl[:, o_sk:o_ga],
                               jnp.zeros((D, wa_pad), F32)], axis=1).astype(BF16)
        w_sq = wl[:, o_sq:o_sk].astype(BF16)
        w_g = wl[:, o_ga:].astype(BF16)
        w_q = jnp.concatenate([w_uq[l][:, :, :nope].reshape(ql, heads * nope),
                               jnp.pad(w_uq[l][:, :, nope:], ((0, 0), (0, 0), (0, LANE - rope_d))).reshape(ql, heads * LANE)],
                              axis=1).astype(BF16)
        w_ukt = w_uk[l].transpose(1, 2, 0).astype(BF16)
        w_uvh = w_uv[l].transpose(1, 0, 2).astype(BF16)

        z_a = _mm(xb, w_a, F32, name="in_proj_a")
        sq = _mm(xb, w_sq, BF16, name="in_proj_sq")
        z_g = _mm(xb, w_g, F32, name="in_proj_gates")
        cqn, mla_rows, kmla, sb_rows, ksb = _rows(z_a, q_norm_g[l][None], kv_norm_g[l][None], cos_t, sin_t, ql, c)
        q = _mm(cqn, w_q, F32, name="q_up")
        q_lat, q_rope = _qbuild(q, w_ukt, cos_t, sin_t, heads, nope, c)

        oa_p, osb_p = _prompt_attention(q_lat, q_rope, sq, kmla, ksb, w_uvh, B, S, heads, c, mla_scale, sb_scale)

        def per_batch(a):
            w = a.shape[1] // heads
            return a[mp:].reshape(NB, T, heads, w).transpose(0, 2, 1, 3).reshape(NB, heads * T, w)

        def new_keys(a):
            return jnp.pad(a[mp:].reshape(NB, T, a.shape[1]), ((0, 0), (0, LANE - T), (0, 0)))

        o_lat_s, o_sb_s = _sample_attention(page_table, l, per_batch(q_lat), per_batch(q_rope), per_batch(sq),
                                            jnp.swapaxes(new_keys(kmla), 1, 2), new_keys(ksb), cache_mla_t, cache_sb,
                                            T, c, mla_scale, sb_scale)
        o_lat_h = o_lat_s.reshape(NB, heads, T, c).transpose(1, 0, 2, 3).reshape(heads, ms, c).astype(BF16)
        oa_s = _uvproj(o_lat_h, w_uvh)
        osb_s = o_sb_s.reshape(NB, heads, T, sbd).transpose(0, 2, 1, 3).reshape(ms, heads * sbd).astype(BF16)

        merged = _merge(jnp.concatenate([oa_p, oa_s], axis=0), jnp.concatenate([osb_p, osb_s], axis=0),
                        w_pa, w_pb, l, z_g)
        mix = _mm(merged, w_o, F32, layer=l, name="out_proj")
        h, hb = _ln_residual(x, mix, ln1_g[l][None], ln1_b[l][None], alpha)
        up = _mm(hb, w_up, BF16, act="relu2", layer=l, name="ffn_up")
        f = _mm(up, w_down_b, F32, layer=l, name="ffn_down")
        x, xb = _ln_residual(h, f, ln2_g[l][None], ln2_b[l][None], alpha)

        outs[0].append(mla_rows[:mp].reshape(B, S, c + rope_d))
        outs[1].append(sb_rows[:mp].reshape(B, S, 2 * sbd))
        outs[2].append(mla_rows[mp:].reshape(NB, T, c + rope_d))
        outs[3].append(sb_rows[mp:].reshape(NB, T, 2 * sbd))

    return (x[:mp].reshape(B, S, D), x[mp:].reshape(NB, T, D),
            jnp.stack(outs[0]), jnp.stack(outs[1]), jnp.stack(outs[2]), jnp.stack(outs[3]))
```

```python
import functools

import jax
import jax.numpy as jnp
from jax import lax
from jax.experimental import pallas as pl
from jax.experimental.pallas import tpu as pltpu

F32 = jnp.float32
BF16 = jnp.bfloat16

ROPE_THETA = 10000.0
LN_EPS = 1e-5
RMS_EPS = 1e-6
NEG_BIG = -1e30

LANE = 128
Q_TILE = 128
K_TILE = 256
HEAD_GROUP = 16
PAGES_PER_STEP = 32
VMEM_LIMIT = 56 * 1024 * 1024
ROW_BLOCKS = (1088, 512, 256, 128, 64, 32, 16, 8)


def _cparams(n_axes):
    return pltpu.CompilerParams(dimension_semantics=("arbitrary",) * n_axes, vmem_limit_bytes=VMEM_LIMIT)


def _pick(n, candidates):
    for c in candidates:
        if n % c == 0:
            return c
    raise ValueError(f"no block size among {candidates} divides {n}")


def _dot(a, b):
    return jnp.dot(a, b, preferred_element_type=F32)


def _dot_nt(a, b):
    return lax.dot_general(a, b, (((1,), (1,)), ((), ())), preferred_element_type=F32)


def _act(y, act):
    return jnp.square(jnp.maximum(y, 0.0)) if act == "relu2" else y


def _mm_kernel(x_ref, w_ref, o_ref, *, act):
    o_ref[...] = _act(_dot(x_ref[...], w_ref[...]), act).astype(o_ref.dtype)


def _mm_cast_kernel(x_ref, w_ref, o_ref, wb_ref, *, act):
    @pl.when(pl.program_id(1) == 0)
    def _():
        wb_ref[...] = w_ref[...].astype(BF16)

    o_ref[...] = _act(_dot(x_ref[...], wb_ref[...]), act).astype(o_ref.dtype)


def _mm_acc_kernel(x_ref, w_ref, o_ref, acc_ref):
    k = pl.program_id(2)

    @pl.when(k == 0)
    def _():
        acc_ref[...] = jnp.zeros_like(acc_ref)

    acc_ref[...] += _dot(x_ref[...], w_ref[...])

    @pl.when(k == pl.num_programs(2) - 1)
    def _():
        o_ref[...] = acc_ref[...].astype(o_ref.dtype)


def _mm(x, w, out_dtype, act=None, layer=None, name="mm"):
    M, K = x.shape
    N = w.shape[-1]
    cast = w.dtype == F32
    bm = _pick(M, ROW_BLOCKS)
    bn = _pick(N, (512, 256, 128) if cast else (1024, 512, 256, 128))
    bk = 4096
    wdims = (K, bn) if layer is None else (None, K, bn)
    if K <= bk:
        widx = (lambda n, m: (0, n)) if layer is None else (lambda n, m: (layer, 0, n))
        return pl.pallas_call(
            functools.partial(_mm_cast_kernel if cast else _mm_kernel, act=act),
            grid=(N // bn, M // bm),
            in_specs=[pl.BlockSpec((bm, K), lambda n, m: (m, 0)), pl.BlockSpec(wdims, widx)],
            out_specs=pl.BlockSpec((bm, bn), lambda n, m: (m, n)),
            out_shape=jax.ShapeDtypeStruct((M, N), out_dtype),
            scratch_shapes=[pltpu.VMEM((K, bn), BF16)] if cast else [],
            compiler_params=_cparams(2), name=name)(x, w)
    assert act is None and K % bk == 0 and not cast
    wdims = (bk, bn) if layer is None else (None, bk, bn)
    widx = (lambda n, m, k: (k, n)) if layer is None else (lambda n, m, k: (layer, k, n))
    return pl.pallas_call(
        _mm_acc_kernel,
        grid=(N // bn, M // bm, K // bk),
        in_specs=[pl.BlockSpec((bm, bk), lambda n, m, k: (m, k)), pl.BlockSpec(wdims, widx)],
        out_specs=pl.BlockSpec((bm, bn), lambda n, m, k: (m, n)),
        out_shape=jax.ShapeDtypeStruct((M, N), out_dtype),
        scratch_shapes=[pltpu.VMEM((bm, bn), F32)],
        compiler_params=_cparams(3), name=name)(x, w)


def _rope_lanes(x, cos, sin):
    lane = lax.broadcasted_iota(jnp.int32, x.shape, 1)
    swapped = jnp.where(lane < 32, pltpu.roll(x, 96, 1), pltpu.roll(x, 32, 1))
    return x * cos + swapped * sin


def _rows_kernel(z_ref, gq_ref, gkv_ref, cos_ref, sin_ref,
                 cqn_ref, mla_ref, kmla_ref, sb_ref, ksb_ref, *, ql, c):
    cq = z_ref[:, :ql]
    cqn = cq * lax.rsqrt(jnp.mean(jnp.square(cq), axis=-1, keepdims=True) + RMS_EPS) * gq_ref[...]
    cqn_ref[...] = cqn.astype(BF16)
    ckv = z_ref[:, ql:ql + c]
    ckvn = ckv * lax.rsqrt(jnp.mean(jnp.square(ckv), axis=-1, keepdims=True) + RMS_EPS) * gkv_ref[...]
    kr = _rope_lanes(z_ref[:, ql + c:ql + c + LANE], cos_ref[...], sin_ref[...])[:, :64]
    mla_ref[:, :c] = ckvn
    mla_ref[:, c:] = kr
    kmla_ref[:, :c] = ckvn.astype(BF16)
    kmla_ref[:, c:] = kr.astype(BF16)
    sb = z_ref[:, ql + c + LANE:ql + c + 3 * LANE]
    sb_ref[...] = sb
    ksb_ref[...] = sb.astype(BF16)


def _rows(z_a, gq, gkv, cos_t, sin_t, ql, c):
    M, wa = z_a.shape
    tm = _pick(M, (256, 128, 64, 32, 16, 8))
    row = lambda w: pl.BlockSpec((tm, w), lambda i: (i, 0))
    full = lambda w: pl.BlockSpec((1, w), lambda i: (0, 0))
    return pl.pallas_call(
        functools.partial(_rows_kernel, ql=ql, c=c),
        grid=(M // tm,),
        in_specs=[row(wa), full(ql), full(c), row(LANE), row(LANE)],
        out_specs=[row(ql), row(c + 64), row(c + 64), row(2 * LANE), row(2 * LANE)],
        out_shape=[jax.ShapeDtypeStruct((M, ql), BF16),
                   jax.ShapeDtypeStruct((M, c + 64), F32),
                   jax.ShapeDtypeStruct((M, c + 64), BF16),
                   jax.ShapeDtypeStruct((M, 2 * LANE), F32),
                   jax.ShapeDtypeStruct((M, 2 * LANE), BF16)],
        compiler_params=_cparams(1), name="rows")(z_a, gq, gkv, cos_t, sin_t)


def _qbuild_kernel(q_ref, wuk_ref, cos_ref, sin_ref, qlat_ref, qrope_ref, *, heads, nope, c):
    cos = cos_ref[...]
    sin = sin_ref[...]
    for h in range(heads):
        qn = q_ref[:, h * nope:(h + 1) * nope].astype(BF16)
        qlat_ref[:, h * c:(h + 1) * c] = _dot(qn, wuk_ref[h]).astype(BF16)
        qr = q_ref[:, heads * nope + h * LANE:heads * nope + (h + 1) * LANE]
        qrope_ref[:, h * LANE:(h + 1) * LANE] = _rope_lanes(qr, cos, sin).astype(BF16)


def _qbuild(q, w_ukt, cos_t, sin_t, heads, nope, c):
    M = q.shape[0]
    tm = _pick(M, (256, 128, 64, 32, 16, 8))
    row = lambda w: pl.BlockSpec((tm, w), lambda i: (i, 0))
    return pl.pallas_call(
        functools.partial(_qbuild_kernel, heads=heads, nope=nope, c=c),
        grid=(M // tm,),
        in_specs=[row(q.shape[1]), pl.BlockSpec(w_ukt.shape, lambda i: (0, 0, 0)), row(LANE), row(LANE)],
        out_specs=[row(heads * c), row(heads * LANE)],
        out_shape=[jax.ShapeDtypeStruct((M, heads * c), BF16),
                   jax.ShapeDtypeStruct((M, heads * LANE), BF16)],
        compiler_params=_cparams(1), name="qbuild")(q, w_ukt, cos_t, sin_t)


def _suffix_matrix(n):
    j = lax.broadcasted_iota(jnp.int32, (n, n), 0)
    s = lax.broadcasted_iota(jnp.int32, (n, n), 1)
    return (j > s).astype(BF16)


def _suffix_sum(x, u):
    hi = x.astype(BF16)
    lo = (x - hi.astype(F32)).astype(BF16)
    return _dot(hi, u) + _dot(lo, u)


def _sb_logs(z, mask):
    t = jnp.log(1.0 + jnp.exp(-jnp.abs(z)))
    log_beta = jnp.minimum(z, 0.0) - t
    log_keep = -jnp.maximum(z, 0.0) - t
    if mask is not None:
        log_keep = jnp.where(mask, log_keep, 0.0)
    return log_beta, log_keep


def _prompt_attn_kernel(qlat_ref, qrope_ref, sq_ref, kmla_ref, ksb_ref, wuv_ref, oa_ref, osb_ref,
                        acc_ref, accsb_ref, *, hg, c, mla_scale, sb_scale):
    i = pl.program_id(1)
    tq, tk = Q_TILE, K_TILE
    mq = hg * tq
    q_lat = jnp.concatenate([qlat_ref[:, h * c:(h + 1) * c] for h in range(hg)], axis=0)
    q_rope = jnp.concatenate([qrope_ref[:, h * LANE:h * LANE + 64] for h in range(hg)], axis=0)
    q_sb = jnp.concatenate([sq_ref[:, h * LANE:(h + 1) * LANE] for h in range(hg)], axis=0)
    q_pos = i * tq + (lax.broadcasted_iota(jnp.int32, (mq, 1), 0) & (tq - 1))
    n_kb = (i * tq + tq - 1) // tk + 1
    u = _suffix_matrix(tk)
    acc_ref[...] = jnp.zeros_like(acc_ref)
    accsb_ref[...] = jnp.zeros_like(accsb_ref)

    def block(kb, masked, carry):
        m, l, run = carry
        start = pl.multiple_of(kb * tk, tk)
        s = (_dot_nt(q_lat, kmla_ref[pl.ds(start, tk), 0:c])
             + _dot_nt(q_rope, kmla_ref[pl.ds(start, tk), c:c + 64])) * mla_scale
        mask = None
        if masked:
            k_pos = start + lax.broadcasted_iota(jnp.int32, (1, tk), 1)
            s = jnp.where(k_pos <= q_pos, s, NEG_BIG)
            mask = k_pos < q_pos
        m_new = jnp.maximum(m, jnp.max(s, axis=-1, keepdims=True))
        alpha = jnp.exp(m - m_new)
        p = jnp.exp(s - m_new)
        l = alpha * l + jnp.sum(p, axis=-1, keepdims=True)
        acc_ref[...] = alpha * acc_ref[...] + _dot(p.astype(BF16), kmla_ref[pl.ds(start, tk), 0:c])
        z = _dot_nt(q_sb, ksb_ref[pl.ds(start, tk), 0:LANE]) * sb_scale
        log_beta, log_keep = _sb_logs(z, mask)
        a = jnp.exp(log_beta + _suffix_sum(log_keep, u) + run)
        if masked:
            a = jnp.where(mask, a, 0.0)
        accsb_ref[...] += _dot(a.astype(BF16), ksb_ref[pl.ds(start, tk), LANE:2 * LANE])
        run = run + jnp.sum(log_keep, axis=-1, keepdims=True)
        return m_new, l, run

    init = (jnp.full((mq, 1), NEG_BIG, F32), jnp.zeros((mq, 1), F32), jnp.zeros((mq, 1), F32))
    carry = block(n_kb - 1, True, init)
    _, l, _ = lax.fori_loop(0, n_kb - 1, lambda it, cr: block(n_kb - 2 - it, False, cr), carry)
    o_lat = (acc_ref[...] / l).astype(BF16)
    for h in range(hg):
        oa_ref[:, h * LANE:(h + 1) * LANE] = _dot(o_lat[h * tq:(h + 1) * tq], wuv_ref[h]).astype(BF16)
        osb_ref[:, h * LANE:(h + 1) * LANE] = accsb_ref[h * tq:(h + 1) * tq, :].astype(BF16)


def _prompt_attention(q_lat, q_rope, sq, kmla, ksb, w_uvh, batch, seq, heads, c, mla_scale, sb_scale):
    hg = _pick(heads, (HEAD_GROUP, 4, 1))
    nq = seq // Q_TILE
    assert seq % K_TILE == 0
    qspec = lambda w: pl.BlockSpec((Q_TILE, hg * w), lambda b, i, g: (b * nq + i, g))
    kspec = lambda w: pl.BlockSpec((seq, w), lambda b, i, g: (b, 0))
    return pl.pallas_call(
        functools.partial(_prompt_attn_kernel, hg=hg, c=c, mla_scale=mla_scale, sb_scale=sb_scale),
        grid=(batch, nq, heads // hg),
        in_specs=[qspec(c), qspec(LANE), qspec(LANE), kspec(c + 64), kspec(2 * LANE),
                  pl.BlockSpec((hg, c, LANE), lambda b, i, g: (g, 0, 0))],
        out_specs=[qspec(LANE), qspec(LANE)],
        out_shape=[jax.ShapeDtypeStruct((batch * seq, heads * LANE), BF16)] * 2,
        scratch_shapes=[pltpu.VMEM((hg * Q_TILE, c), F32), pltpu.VMEM((hg * Q_TILE, LANE), F32)],
        compiler_params=_cparams(3), name="prompt_attn")(q_lat, q_rope, sq, kmla, ksb, w_uvh)


def _sample_attn_kernel(pt_ref, qlat_ref, qrope_ref, sq_ref, newkt_ref, newsb_ref, *rest,
                        pps, c, t_new, mla_scale, sb_scale):
    del pt_ref
    mla_pages = rest[:pps]
    sb_pages = rest[pps:2 * pps]
    olat_ref, osb_ref = rest[2 * pps:2 * pps + 2]
    kbuf, sbuf, m_ref, l_ref, run_ref, acc_ref, accsb_ref = rest[2 * pps + 2:]
    ch = pl.program_id(1)
    page = LANE
    q_lat = qlat_ref[...]
    q_rope = qrope_ref[:, 0:64]
    q_sb = sq_ref[...]
    rows = q_lat.shape[0]
    u = _suffix_matrix(page)

    def update(kt_lat, kt_rope, sb_k, sb_v, n, mla_mask, sb_mask):
        s = (_dot(q_lat, kt_lat) + _dot(q_rope, kt_rope)) * mla_scale
        if mla_mask is not None:
            s = jnp.where(mla_mask, s, NEG_BIG)
        m = m_ref[...]
        m_new = jnp.maximum(m, jnp.max(s, axis=-1, keepdims=True))
        alpha = jnp.exp(m - m_new)
        p = jnp.exp(s - m_new)
        l_ref[...] = alpha * l_ref[...] + jnp.sum(p, axis=-1, keepdims=True)
        m_ref[...] = m_new
        acc_ref[...] = alpha * acc_ref[...] + _dot_nt(p.astype(BF16), kt_lat)
        z = jnp.concatenate([_dot_nt(q_sb, sb_k[j * page:(j + 1) * page]) for j in range(n)], axis=0) * sb_scale
        log_beta, log_keep = _sb_logs(z, sb_mask)
        within = _suffix_sum(log_keep, u)
        tot = jnp.sum(log_keep, axis=-1, keepdims=True)
        run = run_ref[...]
        runs = []
        for j in range(n):
            runs.append(run)
            run = run + tot[j * rows:(j + 1) * rows]
        run_ref[...] = run
        a = jnp.exp(log_beta + within + jnp.concatenate(runs, axis=0))
        if sb_mask is not None:
            a = jnp.where(sb_mask, a, 0.0)
        a = a.astype(BF16)
        upd = _dot(a[0:rows], sb_v[0:page])
        for j in range(1, n):
            upd += _dot(a[j * rows:(j + 1) * rows], sb_v[j * page:(j + 1) * page])
        accsb_ref[...] += upd

    @pl.when(ch == 0)
    def _():
        m_ref[...] = jnp.full_like(m_ref, NEG_BIG)
        l_ref[...] = jnp.zeros_like(l_ref)
        run_ref[...] = jnp.zeros_like(run_ref)
        acc_ref[...] = jnp.zeros_like(acc_ref)
        accsb_ref[...] = jnp.zeros_like(accsb_ref)
        q_t = lax.broadcasted_iota(jnp.int32, (rows, 1), 0) & (t_new - 1)
        k_t = lax.broadcasted_iota(jnp.int32, (1, page), 1)
        update(newkt_ref[0:c, :], newkt_ref[c:c + 64, :], newsb_ref[:, 0:LANE], newsb_ref[:, LANE:2 * LANE], 1,
               (k_t <= q_t) & (k_t < t_new), (k_t < q_t) & (k_t < t_new))

    for j in range(pps):
        kbuf[:, j * page:(j + 1) * page] = mla_pages[j][...].astype(BF16)
        sbuf[j * page:(j + 1) * page, :] = sb_pages[j][...].astype(BF16)
    update(kbuf[0:c, :], kbuf[c:c + 64, :], sbuf[:, 0:LANE], sbuf[:, LANE:2 * LANE], pps, None, None)

    @pl.when(ch == pl.num_programs(1) - 1)
    def _():
        olat_ref[...] = acc_ref[...] / l_ref[...]
        osb_ref[...] = accsb_ref[...]


def _sample_attention(page_table, layer, q_lat, q_rope, sq, newk_t, newsb, cache_mla_t, cache_sb,
                      t_new, c, mla_scale, sb_scale):
    nb, rows, _ = q_lat.shape
    n_pages = page_table.shape[1]
    pps = _pick(n_pages, (PAGES_PER_STEP, 16, 8, 4, 2, 1))
    page = cache_sb.shape[2]
    assert page == LANE and t_new & (t_new - 1) == 0

    def page_spec(r, w, j):
        return pl.BlockSpec((None, None, r, w),
                            lambda b, ch, pt: (layer, pt[b * n_pages + n_pages - 1 - (ch * pps + j)], 0, 0))

    per_b = lambda r, w: pl.BlockSpec((None, r, w), lambda b, ch, pt: (b, 0, 0))
    grid_spec = pltpu.PrefetchScalarGridSpec(
        num_scalar_prefetch=1,
        grid=(nb, n_pages // pps),
        in_specs=[per_b(rows, c), per_b(rows, LANE), per_b(rows, LANE), per_b(c + 64, page), per_b(page, 2 * LANE)]
                 + [page_spec(c + 64, page, j) for j in range(pps)] + [page_spec(page, 2 * LANE, j) for j in range(pps)],
        out_specs=[per_b(rows, c), per_b(rows, LANE)],
        scratch_shapes=[pltpu.VMEM((c + 64, pps * page), BF16), pltpu.VMEM((pps * page, 2 * LANE), BF16),
                        pltpu.VMEM((rows, 1), F32), pltpu.VMEM((rows, 1), F32), pltpu.VMEM((rows, 1), F32),
                        pltpu.VMEM((rows, c), F32), pltpu.VMEM((rows, LANE), F32)])
    return pl.pallas_call(
        functools.partial(_sample_attn_kernel, pps=pps, c=c, t_new=t_new, mla_scale=mla_scale, sb_scale=sb_scale),
        grid_spec=grid_spec,
        out_shape=[jax.ShapeDtypeStruct((nb, rows, c), F32), jax.ShapeDtypeStruct((nb, rows, LANE), F32)],
        compiler_params=_cparams(2), name="sample_attn")(
            page_table.reshape(-1), q_lat, q_rope, sq, newk_t, newsb,
            *([cache_mla_t] * pps), *([cache_sb] * pps))


def _uvproj_kernel(o_ref, w_ref, out_ref):
    out_ref[...] = _dot(o_ref[...], w_ref[...]).astype(out_ref.dtype)


def _uvproj(o_lat_h, w_uvh):
    heads, T, c = o_lat_h.shape
    v = w_uvh.shape[2]
    return pl.pallas_call(
        _uvproj_kernel,
        grid=(heads,),
        in_specs=[pl.BlockSpec((None, T, c), lambda h: (h, 0, 0)), pl.BlockSpec((None, c, v), lambda h: (h, 0, 0))],
        out_specs=pl.BlockSpec((T, v), lambda h: (0, h)),
        out_shape=jax.ShapeDtypeStruct((T, heads * v), BF16),
        compiler_params=_cparams(1), name="uvproj")(o_lat_h, w_uvh)


def _merge_kernel(oa_ref, ob_ref, wpa_ref, wpb_ref, ga_ref, gb_ref, o_ref, wa_ref, wb_ref):
    @pl.when(pl.program_id(1) == 0)
    def _():
        wa_ref[...] = wpa_ref[...].astype(BF16)
        wb_ref[...] = wpb_ref[...].astype(BF16)

    o_a = _dot(oa_ref[...], wa_ref[...])
    o_b = _dot(ob_ref[...], wb_ref[...])
    o_ref[...] = (jax.nn.sigmoid(ga_ref[...]) * o_a + jax.nn.sigmoid(gb_ref[...]) * o_b).astype(o_ref.dtype)


def _merge(oa, ob, w_pa, w_pb, layer, z_g):
    M, ka = oa.shape
    kb = ob.shape[1]
    D = w_pa.shape[2]
    bm = _pick(M, ROW_BLOCKS)
    bn = _pick(D, (512, 256, 128))
    nb = D // bn
    return pl.pallas_call(
        _merge_kernel,
        grid=(nb, M // bm),
        in_specs=[pl.BlockSpec((bm, ka), lambda n, m: (m, 0)), pl.BlockSpec((bm, kb), lambda n, m: (m, 0)),
                  pl.BlockSpec((None, ka, bn), lambda n, m: (layer, 0, n)),
                  pl.BlockSpec((None, kb, bn), lambda n, m: (layer, 0, n)),
                  pl.BlockSpec((bm, bn), lambda n, m: (m, n)), pl.BlockSpec((bm, bn), lambda n, m: (m, n + nb))],
        out_specs=pl.BlockSpec((bm, bn), lambda n, m: (m, n)),
        out_shape=jax.ShapeDtypeStruct((M, D), BF16),
        scratch_shapes=[pltpu.VMEM((ka, bn), BF16), pltpu.VMEM((kb, bn), BF16)],
        compiler_params=_cparams(2), name="merge")(oa, ob, w_pa, w_pb, z_g, z_g)


def _ln_kernel(x_ref, y_ref, g_ref, b_ref, o_ref, ob_ref, *, alpha):
    v = alpha * x_ref[...] + y_ref[...]
    mu = jnp.mean(v, axis=-1, keepdims=True)
    d = v - mu
    var = jnp.mean(jnp.square(d), axis=-1, keepdims=True)
    out = d * lax.rsqrt(var + LN_EPS) * g_ref[...] + b_ref[...]
    o_ref[...] = out
    ob_ref[...] = out.astype(BF16)


def _ln_residual(x, y, g, b, alpha):
    M, D = x.shape
    tm = _pick(M, (128, 64, 32, 16, 8))
    row = pl.BlockSpec((tm, D), lambda i: (i, 0))
    vec = pl.BlockSpec((1, D), lambda i: (0, 0))
    return pl.pallas_call(
        functools.partial(_ln_kernel, alpha=alpha),
        grid=(M // tm,),
        in_specs=[row, row, vec, vec],
        out_specs=[row, row],
        out_shape=[jax.ShapeDtypeStruct((M, D), F32), jax.ShapeDtypeStruct((M, D), BF16)],
        compiler_params=_cparams(1), name="ln_residual")(x, y, g, b)


def _rope_tables(pos):
    inv_freq = 1.0 / (ROPE_THETA ** (jnp.arange(0, 64, 2, dtype=F32) / 64))
    ang = pos.astype(F32)[:, None] * inv_freq[None, :]
    cos, sin = jnp.cos(ang), jnp.sin(ang)
    zeros = jnp.zeros((pos.shape[0], 64), F32)
    return jnp.concatenate([cos, cos, zeros], axis=1), jnp.concatenate([-sin, sin, zeros], axis=1)


def kernel(x_prompt, x_sample, cache_mla, cache_sb, page_table, w_in, q_norm_g, w_uq, kv_norm_g, w_uk, w_uv, w_pa,
           w_pb, w_o, ln1_g, ln1_b, w_up, w_down, ln2_g, ln2_b):
    depth, D, _ = w_in.shape
    B, S, _ = x_prompt.shape
    NB, T, _ = x_sample.shape
    ql = q_norm_g.shape[1]
    c = kv_norm_g.shape[1]
    heads, nope = w_uk.shape[2], w_uk.shape[3]
    rope_d = w_uq.shape[3] - nope
    sbd = cache_sb.shape[-1] // 2
    sb_heads = w_pb.shape[1] // sbd
    assert (c, rope_d, sbd, nope, w_uv.shape[3]) == (256, 64, LANE, LANE, LANE) and sb_heads == heads
    assert S % K_TILE == 0 and cache_mla.shape[-1] == c + rope_d
    past_len = page_table.shape[1] * cache_mla.shape[2]
    alpha = (2 * depth) ** 0.25
    mla_scale = (nope + rope_d) ** -0.5
    sb_scale = sbd ** -0.5
    mp, ms = B * S, NB * T
    M = mp + ms

    o_ckv, o_kr, o_sq = ql, ql + c, ql + c + rope_d
    o_sk = o_sq + heads * sbd
    o_ga = o_sk + 2 * sbd
    wa_cols = ql + c + 3 * LANE
    wa_pad = -wa_cols % 512

    pos = jnp.concatenate([jnp.tile(jnp.arange(S, dtype=jnp.int32), B),
                           jnp.tile(past_len + jnp.arange(T, dtype=jnp.int32), NB)])
    cos_t, sin_t = _rope_tables(pos)

    cache_mla_t = jnp.swapaxes(cache_mla, 2, 3)
    w_down_b = w_down.astype(BF16)

    x = jnp.concatenate([x_prompt.reshape(mp, D), x_sample.reshape(ms, D)], axis=0)
    xb = x.astype(BF16)
    outs = [[], [], [], []]
    for l in range(depth):
        wl = w_in[l]
        w_a = jnp.concatenate([wl[:, :o_sq], jnp.zeros((D, LANE - rope_d), F32), wl[:, o_sk:o_ga],
                               jnp.zeros((D, wa_pad), F32)], axis=1).astype(BF16)
        w_sq = wl[:, o_sq:o_sk].astype(BF16)
        w_g = wl[:, o_ga:].astype(BF16)
        w_q = jnp.concatenate([w_uq[l][:, :, :nope].reshape(ql, heads * nope),
                               jnp.pad(w_uq[l][:, :, nope:], ((0, 0), (0, 0), (0, LANE - rope_d))).reshape(ql, heads * LANE)],
                              axis=1).astype(BF16)
        w_ukt = w_uk[l].transpose(1, 2, 0).astype(BF16)
        w_uvh = w_uv[l].transpose(1, 0, 2).astype(BF16)

        z_a = _mm(xb, w_a, F32, name="in_proj_a")
        sq = _mm(xb, w_sq, BF16, name="in_proj_sq")
        z_g = _mm(xb, w_g, F32, name="in_proj_gates")
        cqn, mla_rows, kmla, sb_rows, ksb = _rows(z_a, q_norm_g[l][None], kv_norm_g[l][None], cos_t, sin_t, ql, c)
        q = _mm(cqn, w_q, F32, name="q_up")
        q_lat, q_rope = _qbuild(q, w_ukt, cos_t, sin_t, heads, nope, c)

        oa_p, osb_p = _prompt_attention(q_lat, q_rope, sq, kmla, ksb, w_uvh, B, S, heads, c, mla_scale, sb_scale)

        def per_batch(a):
            w = a.shape[1] // heads
            return a[mp:].reshape(NB, T, heads, w).transpose(0, 2, 1, 3).reshape(NB, heads * T, w)

        def new_keys(a):
            return jnp.pad(a[mp:].reshape(NB, T, a.shape[1]), ((0, 0), (0, LANE - T), (0, 0)))

        o_lat_s, o_sb_s = _sample_attention(page_table, l, per_batch(q_lat), per_batch(q_rope), per_batch(sq),
                                            jnp.swapaxes(new_keys(kmla), 1, 2), new_keys(ksb), cache_mla_t, cache_sb,
                                            T, c, mla_scale, sb_scale)
        o_lat_h = o_lat_s.reshape(NB, heads, T, c).transpose(1, 0, 2, 3).reshape(heads, ms, c).astype(BF16)
        oa_s = _uvproj(o_lat_h, w_uvh)
        osb_s = o_sb_s.reshape(NB, heads, T, sbd).transpose(0, 2, 1, 3).reshape(ms, heads * sbd).astype(BF16)

        merged = _merge(jnp.concatenate([oa_p, oa_s], axis=0), jnp.concatenate([osb_p, osb_s], axis=0),
                        w_pa, w_pb, l, z_g)
        mix = _mm(merged, w_o, F32, layer=l, name="out_proj")
        h, hb = _ln_residual(x, mix, ln1_g[l][None], ln1_b[l][None], alpha)
        up = _mm(hb, w_up, BF16, act="relu2", layer=l, name="ffn_up")
        f = _mm(up, w_down_b, F32, layer=l, name="ffn_down")
        x, xb = _ln_residual(h, f, ln2_g[l][None], ln2_b[l][None], alpha)

        outs[0].append(mla_rows[:mp].reshape(B, S, c + rope_d))
        outs[1].append(sb_rows[:mp].reshape(B, S, 2 * sbd))
        outs[2].append(mla_rows[mp:].reshape(NB, T, c + rope_d))
        outs[3].append(sb_rows[mp:].reshape(NB, T, 2 * sbd))

    return (x[:mp].reshape(B, S, D), x[mp:].reshape(NB, T, D),
            jnp.stack(outs[0]), jnp.stack(outs[1]), jnp.stack(outs[2]), jnp.stack(outs[3]))
```
